```python
import jax, jax.numpy as jnp
from jax import lax
import numpy as np

D_MODEL = 2048
BATCH = 2
SEQ = 16384
DEPTH = 2

HEAD_DIM = 128
ATTN_GROUPS = ((128, 1), (512, 4), (2048, 16))
HEADS_PER_GROUP = 4
N_ATTN_HEADS = HEADS_PER_GROUP * len(ATTN_GROUPS)
ATTN_WIDTH = N_ATTN_HEADS * HEAD_DIM
ATTN_OUT_WIDTH = HEADS_PER_GROUP * HEAD_DIM
BAND_BLOCK = 128
ROPE_THETA = 10000.0
POOL_WINDOWS = (2, 4, 8, 16)
POOL_WIDTH = D_MODEL // 2
POOL_GROUP_WIDTH = POOL_WIDTH // len(POOL_WINDOWS)
N_BRANCHES = 2
IN_WIDTH = 3 * ATTN_WIDTH + POOL_WIDTH + N_BRANCHES * D_MODEL
PEER_HEADS = 8
PEER_N_KEYS = 128
PEER_N_EXPERTS = PEER_N_KEYS * PEER_N_KEYS
PEER_KEY_DIM = 256
PEER_HALF = PEER_KEY_DIM // 2
PEER_TOPK = 16
PEER_TOKEN_BLOCK = 128
LN_EPS = 1e-5
DEEPNORM_ALPHA = (2 * DEPTH) ** 0.25
DEEPNORM_BETA = (8 * DEPTH) ** -0.25

kernel_name = "hybrid_dilated_pool_peer_deepnorm"


def layer_norm(x, g, b):
    xf = x.astype(jnp.float32)
    mu = xf.mean(-1, keepdims=True)
    var = jnp.square(xf - mu).mean(-1, keepdims=True)
    return ((xf - mu) * lax.rsqrt(var + LN_EPS) * g.astype(jnp.float32) + b.astype(jnp.float32)).astype(x.dtype)


def rope_tables(seq_len):
    inv = ROPE_THETA ** (-jnp.arange(0, HEAD_DIM, 2, dtype=jnp.float32) / HEAD_DIM)
    ang = jnp.arange(seq_len, dtype=jnp.float32)[:, None] * inv[None, :]
    ang = jnp.concatenate([ang, ang], axis=-1)
    return jnp.cos(ang), jnp.sin(ang)


def apply_rope(t, cos, sin):
    t32 = t.astype(jnp.float32)
    t1, t2 = jnp.split(t32, 2, axis=-1)
    rot = jnp.concatenate([-t2, t1], axis=-1)
    return (t32 * cos[:, None, :] + rot * sin[:, None, :]).astype(t.dtype)


def dilated_window_attention(q, k, v, window, dilation):
    B, S, H, Dh = q.shape
    r = dilation
    n_back = window // dilation
    L = S // r
    nblk = -(-L // BAND_BLOCK)
    Lp = nblk * BAND_BLOCK

    def to_strided(t):
        t = t.reshape(B, L, r, H, Dh).transpose(0, 2, 1, 3, 4)
        return jnp.pad(t, ((0, 0), (0, 0), (0, Lp - L), (0, 0), (0, 0)))

    qs, ks, vs = to_strided(q), to_strided(k), to_strided(v)
    qb = qs.reshape(B, r, nblk, BAND_BLOCK, H, Dh)

    def band_keys(t):
        prev = jnp.pad(t, ((0, 0), (0, 0), (BAND_BLOCK, 0), (0, 0), (0, 0)))[:, :, :Lp]
        return jnp.concatenate([prev.reshape(B, r, nblk, BAND_BLOCK, H, Dh),
                                t.reshape(B, r, nblk, BAND_BLOCK, H, Dh)], axis=3)

    kb, vb = band_keys(ks), band_keys(vs)
    s = jnp.einsum('brnqhd,brnkhd->brnhqk', qb, kb, preferred_element_type=jnp.float32) * (Dh ** -0.5)
    qa = jnp.arange(BAND_BLOCK)[:, None]
    kbi = jnp.arange(2 * BAND_BLOCK)[None, :]
    dist = qa + BAND_BLOCK - kbi
    band = (dist >= 0) & (dist <= n_back)
    blk = jnp.arange(nblk)[:, None, None]
    mask = band[None] & ((blk > 0) | (kbi >= BAND_BLOCK)[None])
    s = jnp.where(mask[None, None, :, None], s, -jnp.inf)
    m = s.max(-1, keepdims=True)
    p = jnp.exp(s - m)
    den = p.sum(-1, keepdims=True)
    o = jnp.einsum('brnhqk,brnkhd->brnqhd', (p * (1.0 / den)).astype(v.dtype), vb)
    lse = (m + jnp.log(den))[..., 0]

    def from_strided(t):
        t = t[:, :, :L]
        return jnp.moveaxis(t, 1, 2).reshape((B, S) + t.shape[3:])

    o = from_strided(o.reshape(B, r, Lp, H, Dh))
    lse = from_strided(jnp.swapaxes(lse, -1, -2).reshape(B, r, Lp, H))
    return o, lse


def multiscale_pool(xp, w_pool_group, pool_scale):
    B, S, _ = xp.shape
    xf = xp.reshape(B, S, len(POOL_WINDOWS), POOL_GROUP_WIDTH).astype(jnp.float32)
    cs = jnp.cumsum(xf, axis=1)
    t = jnp.arange(S, dtype=jnp.float32)
    outs = []
    for g, w in enumerate(POOL_WINDOWS):
        c = cs[:, :, g]
        c_shift = jnp.pad(c, ((0, 0), (w, 0), (0, 0)))[:, :S]
        count = jnp.minimum(t + 1.0, float(w))
        outs.append((c - c_shift) / count[None, :, None] - xf[:, :, g])
    pooled = jnp.stack(outs, axis=2).astype(xp.dtype)
    mixed = jnp.einsum('bsgc,gcd->bsgd', pooled, w_pool_group)
    return mixed.reshape(B, S, POOL_WIDTH) * pool_scale


def hybrid_mixer(x, w_in, b_gate, w_branch_attn, w_branch_pool, w_pool_group, pool_scale, w_out, cos, sin):
    B, S, _ = x.shape
    z = x @ w_in
    q, k, v, xp, gl = jnp.split(z, [ATTN_WIDTH, 2 * ATTN_WIDTH, 3 * ATTN_WIDTH,
                                    3 * ATTN_WIDTH + POOL_WIDTH], axis=-1)
    q = apply_rope(q.reshape(B, S, N_ATTN_HEADS, HEAD_DIM), cos, sin)
    k = apply_rope(k.reshape(B, S, N_ATTN_HEADS, HEAD_DIM), cos, sin)
    v = v.reshape(B, S, N_ATTN_HEADS, HEAD_DIM)
    outs, lses = [], []
    for g, (window, dilation) in enumerate(ATTN_GROUPS):
        sl = slice(g * HEADS_PER_GROUP, (g + 1) * HEADS_PER_GROUP)
        o, l = dilated_window_attention(q[:, :, sl], k[:, :, sl], v[:, :, sl], window, dilation)
        outs.append(o)
        lses.append(l)
    wts = jax.nn.softmax(jnp.stack(lses, axis=0), axis=0)
    attn = jnp.einsum('gbsh,gbshd->bshd', wts.astype(x.dtype), jnp.stack(outs, axis=0))
    branch_a = attn.reshape(B, S, ATTN_OUT_WIDTH) @ w_branch_attn
    branch_b = multiscale_pool(xp, w_pool_group, pool_scale) @ w_branch_pool
    gates = jax.nn.sigmoid(gl.reshape(B, S, N_BRANCHES, D_MODEL) + b_gate)
    merged = gates[:, :, 0] * branch_a + gates[:, :, 1] * branch_b
    return merged @ w_out


def peer_ffn(x, w_peer_q, peer_subkeys, peer_u, peer_v):
    B, S, D = x.shape
    T = PEER_TOKEN_BLOCK
    xb = x.reshape(B * S // T, T, D)

    def block(xt):
        q = (xt @ w_peer_q).reshape(T, PEER_HEADS, 2, PEER_HALF)
        s = jnp.einsum('thcd,hckd->thck', q, peer_subkeys, preferred_element_type=jnp.float32)
        sv, si = lax.top_k(s, PEER_TOPK)
        cand = sv[:, :, 0, :, None] + sv[:, :, 1, None, :]
        cv, ci = lax.top_k(cand.reshape(T, PEER_HEADS, PEER_TOPK * PEER_TOPK), PEER_TOPK)
        i1 = jnp.take_along_axis(si[:, :, 0], ci // PEER_TOPK, axis=-1)
        i2 = jnp.take_along_axis(si[:, :, 1], ci % PEER_TOPK, axis=-1)
        experts = i1 * PEER_N_KEYS + i2
        gate = jax.nn.softmax(cv, axis=-1).astype(xt.dtype)
        h = jnp.einsum('thkd,td->thk', peer_u[experts], xt)
        act = gate * jax.nn.gelu(h, approximate=False)
        return jnp.einsum('thk,thkd->td', act, peer_v[experts])

    return lax.map(block, xb).reshape(B, S, D)


def setup_inputs(seed: int = 0) -> dict:
    key = jax.random.key(seed)
    ks = jax.random.split(key, 17)
    L = DEPTH
    nrm = jax.random.normal
    f32 = jnp.float32
    return {
        "x": nrm(ks[0], (BATCH, SEQ, D_MODEL), f32),
        "w_in": nrm(ks[1], (L, D_MODEL, IN_WIDTH), f32) * D_MODEL ** -0.5,
        "b_gate": nrm(ks[2], (L, N_BRANCHES, D_MODEL), f32) * 0.02,
        "w_branch_attn": nrm(ks[3], (L, ATTN_OUT_WIDTH, D_MODEL), f32) * ATTN_OUT_WIDTH ** -0.5,
        "w_branch_pool": nrm(ks[4], (L, POOL_WIDTH, D_MODEL), f32) * POOL_WIDTH ** -0.5,
        "w_pool_group": nrm(ks[5], (L, len(POOL_WINDOWS), POOL_GROUP_WIDTH, POOL_GROUP_WIDTH), f32) * POOL_GROUP_WIDTH ** -0.5,
        "pool_scale": 1.0 + 0.02 * nrm(ks[6], (L, POOL_WIDTH), f32),
        "w_out": nrm(ks[7], (L, D_MODEL, D_MODEL), f32) * (D_MODEL ** -0.5 * DEEPNORM_BETA),
        "ln1_g": 1.0 + 0.02 * nrm(ks[8], (L, D_MODEL), f32),
        "ln1_b": 0.02 * nrm(ks[9], (L, D_MODEL), f32),
        "w_peer_q": nrm(ks[10], (L, D_MODEL, PEER_HEADS * PEER_KEY_DIM), f32) * D_MODEL ** -0.5,
        "peer_subkeys": nrm(ks[11], (L, PEER_HEADS, 2, PEER_N_KEYS, PEER_HALF), f32) * PEER_HALF ** -0.5,
        "peer_u": nrm(ks[12], (L, PEER_N_EXPERTS, D_MODEL), f32) * D_MODEL ** -0.5,
        "peer_v": nrm(ks[13], (L, PEER_N_EXPERTS, D_MODEL), f32) * (DEEPNORM_BETA * PEER_HEADS ** -0.5),
        "ln2_g": 1.0 + 0.02 * nrm(ks[14], (L, D_MODEL), f32),
        "ln2_b": 0.02 * nrm(ks[15], (L, D_MODEL), f32),
    }


def reference(x, w_in, b_gate, w_branch_attn, w_branch_pool, w_pool_group, pool_scale, w_out,
              ln1_g, ln1_b, w_peer_q, peer_subkeys, peer_u, peer_v, ln2_g, ln2_b):
    cos, sin = rope_tables(x.shape[1])
    for l in range(DEPTH):
        y = hybrid_mixer(x, w_in[l], b_gate[l], w_branch_attn[l], w_branch_pool[l],
                         w_pool_group[l], pool_scale[l], w_out[l], cos, sin)
        x = layer_norm(DEEPNORM_ALPHA * x + y, ln1_g[l], ln1_b[l])
        y = peer_ffn(x, w_peer_q[l], peer_subkeys[l], peer_u[l], peer_v[l])
        x = layer_norm(DEEPNORM_ALPHA * x + y, ln2_g[l], ln2_b[l])
    return x
```

```python
import functools

import jax
import jax.numpy as jnp
from jax import lax
from jax.experimental import pallas as pl
from jax.experimental.pallas import tpu as pltpu

F32 = jnp.float32
BF16 = jnp.bfloat16

HEAD_DIM = 128
ATTN_DILATIONS = (1, 4, 16)
HEADS_PER_GROUP = 4
GROUP_WIDTH = HEADS_PER_GROUP * HEAD_DIM
ATTN_WIDTH = len(ATTN_DILATIONS) * GROUP_WIDTH
BAND = 128
ROPE_THETA = 10000.0
POOL_WINDOWS = (2, 4, 8, 16)
POOL_HALO = 16
PEER_HEADS = 8
PEER_N_KEYS = 128
PEER_HALF = 128
PEER_TOPK = 16
LN_EPS = 1e-5
DEPTH = 2
DEEPNORM_ALPHA = (2 * DEPTH) ** 0.25
NEG = -1e30

VMEM_LIMIT = 56 * 1024 * 1024
PROJ_ROWS = 1024
MIX_ROWS = 512
ATTN_ROWS = 512
ROUTE_TOKENS = 512
EXPERT_TOKENS = 512
EXPERT_CHUNK = 1024


def _params(*sem):
    return pltpu.CompilerParams(dimension_semantics=sem, vmem_limit_bytes=VMEM_LIMIT)


def _dot(a, b):
    return jnp.dot(a, b, preferred_element_type=F32)


def _dot_nt(a, b):
    return lax.dot_general(a, b, (((1,), (1,)), ((), ())), preferred_element_type=F32)


def _dot_tn(a, b):
    return lax.dot_general(a, b, (((0,), (0,)), ((), ())), preferred_element_type=F32)


def _layer_norm(z, g, b):
    mu = jnp.mean(z, axis=-1, keepdims=True)
    zc = z - mu
    var = jnp.mean(zc * zc, axis=-1, keepdims=True)
    return zc * lax.rsqrt(var + LN_EPS) * g + b


def _qkv_kernel(x_ref, w_ref, cos_ref, sin_ref, q_ref, k_ref, v_ref, acc_ref, *, r, rows):
    j = pl.program_id(1)
    res = _dot(x_ref[...], w_ref[...])
    for h in range(HEADS_PER_GROUP):
        acc_ref[h] = res[:, h * HEAD_DIM:(h + 1) * HEAD_DIM]
    n = rows // r

    def phase_rows(ref, p):
        return ref[...] if r == 1 else ref[pl.ds(p, n, r), :]

    def store(o_ref, rope, scale):
        for p in range(r):
            if rope:
                c = phase_rows(cos_ref, p)
                s = phase_rows(sin_ref, p)
            for h in range(HEADS_PER_GROUP):
                t = phase_rows(acc_ref.at[h], p)
                if rope:
                    t = t * c + pltpu.roll(t, HEAD_DIM // 2, 1) * s
                if scale != 1.0:
                    t = t * scale
                o_ref[0, p, :, h * HEAD_DIM:(h + 1) * HEAD_DIM] = t.astype(o_ref.dtype)

    @pl.when(j == 0)
    def _():
        store(q_ref, True, HEAD_DIM ** -0.5)

    @pl.when(j == 1)
    def _():
        store(k_ref, True, 1.0)

    @pl.when(j == 2)
    def _():
        store(v_ref, False, 1.0)


def _qkv_proj(xb, w_in, cos, sin_signed, group, batch, seq):
    n_tok, d = xb.shape
    r = ATTN_DILATIONS[group]
    rows = min(PROJ_ROWS, seq)
    tps = seq // rows
    n_groups = len(ATTN_DILATIONS)
    out = jax.ShapeDtypeStruct((batch, r, seq // r, GROUP_WIDTH), BF16)
    o_spec = pl.BlockSpec((1, r, rows // r, GROUP_WIDTH), lambda i, j: (i // tps, 0, i % tps, 0))
    return pl.pallas_call(
        functools.partial(_qkv_kernel, r=r, rows=rows),
        grid=(n_tok // rows, 3),
        in_specs=[
            pl.BlockSpec((rows, d), lambda i, j: (i, 0)),
            pl.BlockSpec((d, GROUP_WIDTH), lambda i, j: (0, group + n_groups * j)),
            pl.BlockSpec((rows, HEAD_DIM), lambda i, j: (i % tps, 0)),
            pl.BlockSpec((rows, HEAD_DIM), lambda i, j: (i % tps, 0)),
        ],
        out_specs=[o_spec, o_spec, o_spec],
        out_shape=[out, out, out],
        scratch_shapes=[pltpu.VMEM((HEADS_PER_GROUP, rows, HEAD_DIM), F32)],
        compiler_params=_params("arbitrary", "arbitrary"),
        name=f"qkv_proj_g{group}",
    )(xb, w_in, cos, sin_signed)


def _pool_kernel(x_ref, w_ref, o_ref, ext_ref, *, rows, tps):
    i = pl.program_id(0)
    it = i % tps

    @pl.when(it == 0)
    def _():
        ext_ref[0:POOL_HALO, :] = jnp.zeros((POOL_HALO, ext_ref.shape[1]), F32)

    ext_ref[POOL_HALO:POOL_HALO + rows, :] = _dot(x_ref[...], w_ref[...])
    pos = (it * rows + lax.broadcasted_iota(jnp.int32, (rows, 1), 0)).astype(F32)
    gw = ext_ref.shape[1] // len(POOL_WINDOWS)
    for g, w in enumerate(POOL_WINDOWS):
        cols = slice(g * gw, (g + 1) * gw)
        tok = ext_ref[POOL_HALO:POOL_HALO + rows, cols]
        acc = tok
        for dlt in range(1, w):
            acc = acc + ext_ref[POOL_HALO - dlt:POOL_HALO - dlt + rows, cols]
        inv = 1.0 / jnp.minimum(pos + 1.0, float(w))
        o_ref[:, cols] = (acc * inv - tok).astype(o_ref.dtype)
    ext_ref[0:POOL_HALO, :] = ext_ref[rows:rows + POOL_HALO, :]


def _pool_proj(xb, w_pool, seq):
    n_tok, d = xb.shape
    pw = w_pool.shape[1]
    rows = min(PROJ_ROWS, seq)
    return pl.pallas_call(
        functools.partial(_pool_kernel, rows=rows, tps=seq // rows),
        grid=(n_tok // rows,),
        in_specs=[
            pl.BlockSpec((rows, d), lambda i: (i, 0)),
            pl.BlockSpec((d, pw), lambda i: (0, 0)),
        ],
        out_specs=pl.BlockSpec((rows, pw), lambda i: (i, 0)),
        out_shape=jax.ShapeDtypeStruct((n_tok, pw), BF16),
        scratch_shapes=[pltpu.VMEM((rows + POOL_HALO, pw), F32)],
        compiler_params=_params("arbitrary"),
        name="pool_proj",
    )(xb, w_pool)


def _gate_kernel(x_ref, w_ref, b_ref, o_ref):
    z = _dot(x_ref[...], w_ref[...]) + b_ref[...]
    o_ref[...] = jax.nn.sigmoid(z).astype(o_ref.dtype)


def _gate_proj(xb, w_in, b_gate_row):
    n_tok, d = xb.shape
    rows = min(PROJ_ROWS, n_tok)
    tn = 512
    col0 = (3 * ATTN_WIDTH + d // 2) // tn
    assert col0 * tn == 3 * ATTN_WIDTH + d // 2 and (2 * d) % tn == 0
    return pl.pallas_call(
        _gate_kernel,
        grid=(n_tok // rows, 2 * d // tn),
        in_specs=[
            pl.BlockSpec((rows, d), lambda i, j: (i, 0)),
            pl.BlockSpec((d, tn), lambda i, j: (0, col0 + j)),
            pl.BlockSpec((1, tn), lambda i, j: (0, j)),
        ],
        out_specs=pl.BlockSpec((rows, tn), lambda i, j: (i, j)),
        out_shape=jax.ShapeDtypeStruct((n_tok, 2 * d), BF16),
        compiler_params=_params("arbitrary", "arbitrary"),
        name="gate_proj",
    )(xb, w_in, b_gate_row)


def _attn_kernel(q_ref, kp_ref, kc_ref, vp_ref, vc_ref, o_ref, lse_ref, kf_ref, vf_ref, *, qb):
    n = pl.program_id(2)
    kf_ref[0:BAND, :] = kp_ref[0, 0]
    kf_ref[BAND:BAND + qb, :] = kc_ref[0, 0]
    vf_ref[0:BAND, :] = vp_ref[0, 0]
    vf_ref[BAND:BAND + qb, :] = vc_ref[0, 0]
    row = lax.broadcasted_iota(jnp.int32, (BAND, 2 * BAND), 0)
    col = lax.broadcasted_iota(jnp.int32, (BAND, 2 * BAND), 1)
    dist = row + BAND - col
    band = (dist >= 0) & (dist <= BAND)
    first = band & (col >= jnp.where(n > 0, 0, BAND))
    lane = lax.broadcasted_iota(jnp.int32, (BAND, HEAD_DIM), 1)
    for t in range(qb // BAND):
        mask = first if t == 0 else band
        lse_blk = jnp.zeros((BAND, HEAD_DIM), F32)
        for h in range(HEADS_PER_GROUP):
            cols = slice(h * HEAD_DIM, (h + 1) * HEAD_DIM)
            q = q_ref[0, 0, t * BAND:(t + 1) * BAND, cols]
            kk = kf_ref[t * BAND:(t + 2) * BAND, cols]
            vv = vf_ref[t * BAND:(t + 2) * BAND, cols]
            s = jnp.where(mask, _dot_nt(q, kk), NEG)
            m = jnp.max(s, axis=-1, keepdims=True)
            p = jnp.exp(s - m)
            den = jnp.sum(p, axis=-1, keepdims=True)
            o = _dot(p.astype(BF16), vv) * (1.0 / den)
            o_ref[0, 0, t * BAND:(t + 1) * BAND, cols] = o.astype(o_ref.dtype)
            lse_blk = jnp.where(lane == h, m + jnp.log(den), lse_blk)
        lse_ref[0, 0, t * BAND:(t + 1) * BAND, :] = lse_blk


def _band_attention(q, k, v):
    batch, r, length, _ = q.shape
    qb = min(ATTN_ROWS, length)
    sub = qb // BAND
    cur = pl.BlockSpec((1, 1, qb, GROUP_WIDTH), lambda b, p, n: (b, p, n, 0))
    prev = pl.BlockSpec((1, 1, BAND, GROUP_WIDTH), lambda b, p, n: (b, p, jnp.maximum(n * sub - 1, 0), 0))
    return pl.pallas_call(
        functools.partial(_attn_kernel, qb=qb),
        grid=(batch, r, length // qb),
        in_specs=[cur, prev, cur, prev, cur],
        out_specs=[cur, pl.BlockSpec((1, 1, qb, HEAD_DIM), lambda b, p, n: (b, p, n, 0))],
        out_shape=[jax.ShapeDtypeStruct(q.shape, BF16),
                   jax.ShapeDtypeStruct((batch, r, length, HEAD_DIM), F32)],
        scratch_shapes=[pltpu.VMEM((BAND + qb, GROUP_WIDTH), BF16),
                        pltpu.VMEM((BAND + qb, GROUP_WIDTH), BF16)],
        compiler_params=_params("arbitrary", "arbitrary", "arbitrary"),
        name=f"band_attention_r{r}",
    )(q, k, k, v, v)


def _merge_kernel(o0_ref, o1_ref, o2_ref, l0_ref, l1_ref, l2_ref, pooled_ref, gates_ref,
                  wa_ref, wg_ref, ps_ref, wb_ref, out_ref, on1_ref, on2_ref, ln1_ref, ln2_ref, *, rows):
    for o_ref, l_ref, on_ref, ln_ref, r in ((o1_ref, l1_ref, on1_ref, ln1_ref, ATTN_DILATIONS[1]),
                                            (o2_ref, l2_ref, on2_ref, ln2_ref, ATTN_DILATIONS[2])):
        n = rows // r
        for p in range(r):
            op = o_ref[0, p].astype(F32)
            for h in range(HEADS_PER_GROUP):
                on_ref[h, pl.ds(p, n, r), :] = op[:, h * HEAD_DIM:(h + 1) * HEAD_DIM]
            ln_ref[pl.ds(p, n, r), :] = l_ref[0, p]
    l0, l1, l2 = l0_ref[0, 0], ln1_ref[...], ln2_ref[...]
    m = jnp.maximum(jnp.maximum(l0, l1), l2)
    e0, e1, e2 = jnp.exp(l0 - m), jnp.exp(l1 - m), jnp.exp(l2 - m)
    inv = 1.0 / (e0 + e1 + e2)
    w0, w1, w2 = e0 * inv, e1 * inv, e2 * inv
    parts = []
    for h in range(HEADS_PER_GROUP):
        cols = slice(h * HEAD_DIM, (h + 1) * HEAD_DIM)
        parts.append(w0[:, h:h + 1] * o0_ref[0, 0, :, cols].astype(F32)
                     + w1[:, h:h + 1] * on1_ref[h]
                     + w2[:, h:h + 1] * on2_ref[h])
    attn = jnp.concatenate(parts, axis=-1).astype(BF16)
    branch_a = _dot(attn, wa_ref[...])
    gw = wg_ref.shape[1]
    mixed = jnp.concatenate(
        [_dot(pooled_ref[:, g * gw:(g + 1) * gw], wg_ref[g]) for g in range(len(POOL_WINDOWS))], axis=-1)
    branch_b = _dot((mixed * ps_ref[...]).astype(BF16), wb_ref[...])
    d = branch_a.shape[1]
    merged = gates_ref[:, 0:d].astype(F32) * branch_a + gates_ref[:, d:2 * d].astype(F32) * branch_b
    out_ref[...] = merged.astype(out_ref.dtype)


def _merge(outs, lses, pooled, gates, w_ba, w_pg, pool_scale_row, w_bp, batch, seq):
    n_tok = pooled.shape[0]
    d = w_ba.shape[1]
    rows = min(MIX_ROWS, seq)
    tps = seq // rows

    def strided(r, width):
        return pl.BlockSpec((1, r, rows // r, width), lambda i: (i // tps, 0, i % tps, 0))

    def whole(a):
        return pl.BlockSpec(a.shape, lambda i: (0,) * a.ndim)

    return pl.pallas_call(
        functools.partial(_merge_kernel, rows=rows),
        grid=(n_tok // rows,),
        in_specs=[strided(r, GROUP_WIDTH) for r in ATTN_DILATIONS]
        + [strided(r, HEAD_DIM) for r in ATTN_DILATIONS]
        + [pl.BlockSpec((rows, pooled.shape[1]), lambda i: (i, 0)),
           pl.BlockSpec((rows, gates.shape[1]), lambda i: (i, 0)),
           whole(w_ba), whole(w_pg), whole(pool_scale_row), whole(w_bp)],
        out_specs=pl.BlockSpec((rows, d), lambda i: (i, 0)),
        out_shape=jax.ShapeDtypeStruct((n_tok, d), BF16),
        scratch_shapes=[pltpu.VMEM((HEADS_PER_GROUP, rows, HEAD_DIM), F32),
                        pltpu.VMEM((HEADS_PER_GROUP, rows, HEAD_DIM), F32),
                        pltpu.VMEM((rows, HEAD_DIM), F32), pltpu.VMEM((rows, HEAD_DIM), F32)],
        compiler_params=_params("arbitrary"),
        name="merge",
    )(*outs, *lses, pooled, gates, w_ba, w_pg, pool_scale_row, w_bp)


def _out_norm_kernel(m_ref, x_ref, w_ref, g_ref, b_ref, o_ref, ob_ref):
    z = DEEPNORM_ALPHA * x_ref[...] + _dot(m_ref[...], w_ref[...])
    out = _layer_norm(z, g_ref[...], b_ref[...])
    o_ref[...] = out
    ob_ref[...] = out.astype(ob_ref.dtype)


def _out_norm(merged, x, w_out, g_row, b_row):
    n_tok, d = x.shape
    rows = min(MIX_ROWS, n_tok)
    tile = pl.BlockSpec((rows, d), lambda i: (i, 0))
    vec = pl.BlockSpec((1, d), lambda i: (0, 0))
    return pl.pallas_call(
        _out_norm_kernel,
        grid=(n_tok // rows,),
        in_specs=[tile, tile, pl.BlockSpec((d, d), lambda i: (0, 0)), vec, vec],
        out_specs=[tile, tile],
        out_shape=[jax.ShapeDtypeStruct((n_tok, d), F32), jax.ShapeDtypeStruct((n_tok, d), BF16)],
        compiler_params=_params("arbitrary"),
        name="out_norm",
    )(merged, x, w_out, g_row, b_row)


def _top_rows(s, k):
    nrows, t = s.shape
    rid = lax.broadcasted_iota(jnp.int32, s.shape, 0).astype(F32)
    kid = lax.broadcasted_iota(jnp.int32, (k, t), 0)

    def body(i, carry):
        cur, rank, vals = carry
        mx = jnp.max(cur, axis=0, keepdims=True)
        first = jnp.min(jnp.where(cur == mx, rid, float(nrows)), axis=0, keepdims=True)
        sel = rid == first
        rank = jnp.where(sel, i.astype(F32), rank)
        cur = jnp.where(sel, -jnp.inf, cur)
        vals = jnp.where(kid == i, mx, vals)
        return cur, rank, vals

    init = (s, jnp.full(s.shape, float(k), F32), jnp.zeros((k, t), F32))
    _, rank, vals = lax.fori_loop(0, k, body, init)
    return vals, rank


def _pair_rows(p, q, op, fill):
    t = p.shape[1]
    row8 = lax.broadcasted_iota(jnp.int32, (8, t), 0)
    pieces = [op(p[0:1], q[0:16]), op(p[1:2], q[0:8])]
    for i in range(2, 8):
        pieces.append(jnp.where(row8 < PEER_TOPK // (i + 1), op(p[i:i + 1], q[0:8]), fill))
    pieces.append(op(p[8:16], q[0:1]))
    return jnp.concatenate(pieces, axis=0)


def _route_kernel(x_ref, wq_ref, sk_ref, r2_ref, e2_ref, n1_ref, e1_ref):
    assert PEER_TOPK == 16
    qf = _dot(x_ref[...], wq_ref[...]).astype(BF16)
    s1 = _dot_nt(sk_ref[0, 0], qf[:, 0:PEER_HALF])
    s2 = _dot_nt(sk_ref[0, 1], qf[:, PEER_HALF:2 * PEER_HALF])
    p_val, rank1 = _top_rows(s1, PEER_TOPK)
    q_val, rank2 = _top_rows(s2, PEER_TOPK)
    cand = _pair_rows(p_val, q_val, jnp.add, -jnp.inf)
    _, crank = _top_rows(cand, PEER_TOPK)
    chosen = jnp.where(crank < float(PEER_TOPK), 1.0, 0.0)
    ep = jnp.exp(p_val - p_val[0:1])
    eq = jnp.exp(q_val - q_val[0:1])
    z = jnp.sum(chosen * _pair_rows(ep, eq, jnp.multiply, 0.0), axis=0, keepdims=True)
    t = s1.shape[1]
    row8 = lax.broadcasted_iota(jnp.int32, (8, t), 0)
    head = jnp.zeros((8, t), F32)
    head = jnp.where(row8 == 0, jnp.sum(chosen[0:16], axis=0, keepdims=True), head)
    for i in range(1, 8):
        head = jnp.where(row8 == i, jnp.sum(chosen[8 + 8 * i:16 + 8 * i], axis=0, keepdims=True), head)
    nsel_sorted = jnp.concatenate([head, chosen[72:80]], axis=0)
    nsel = jnp.zeros(s1.shape, F32)
    for i in range(PEER_TOPK):
        nsel = jnp.where(rank1 == float(i), nsel_sorted[i:i + 1], nsel)
    r2_ref[0] = rank2
    e2_ref[0] = jnp.exp(s2 - q_val[0:1])
    n1_ref[0] = nsel
    e1_ref[0] = jnp.exp(s1 - p_val[0:1]) * (1.0 / z)


def _route(xb, w_q, subkeys):
    n_tok, d = xb.shape
    tt = min(ROUTE_TOKENS, n_tok)
    out = jax.ShapeDtypeStruct((PEER_HEADS, PEER_N_KEYS, n_tok), F32)
    o_spec = pl.BlockSpec((1, PEER_N_KEYS, tt), lambda i, h: (h, 0, i))
    return pl.pallas_call(
        _route_kernel,
        grid=(n_tok // tt, PEER_HEADS),
        in_specs=[
            pl.BlockSpec((tt, d), lambda i, h: (i, 0)),
            pl.BlockSpec((d, 2 * PEER_HALF), lambda i, h: (0, h)),
            pl.BlockSpec((1, 2, PEER_N_KEYS, PEER_HALF), lambda i, h: (h, 0, 0, 0)),
        ],
        out_specs=[o_spec] * 4,
        out_shape=[out] * 4,
        compiler_params=_params("arbitrary", "arbitrary"),
        name="peer_route",
    )(xb, w_q, subkeys)


def _expert_kernel(x_ref, u_ref, v_ref, r2_ref, e2_ref, n1_ref, e1_ref, y_ref, at_ref):
    c = pl.program_id(1)

    @pl.when(c == 0)
    def _():
        y_ref[...] = jnp.zeros(y_ref.shape, F32)

    ht = _dot_nt(u_ref[...], x_ref[...])
    for r in range(u_ref.shape[0] // PEER_N_KEYS):
        rows = slice(r * PEER_N_KEYS, (r + 1) * PEER_N_KEYS)
        gate = jnp.zeros((PEER_N_KEYS, ht.shape[1]), F32)
        for h in range(PEER_HEADS):
            keep = r2_ref[h] < n1_ref[h, r:r + 1, :]
            gate = gate + jnp.where(keep, e2_ref[h], 0.0) * e1_ref[h, r:r + 1, :]
        hr = ht[rows]
        act = 0.5 * hr * (1.0 + lax.erf(hr * (2.0 ** -0.5)))
        at_ref[rows, :] = (gate * act).astype(at_ref.dtype)
    y_ref[...] += _dot_tn(at_ref[...], v_ref[...])


def _experts(xb, u, v, rank2, e2, nsel, e1):
    n_tok, d = xb.shape
    n_exp = u.shape[0]
    tt = min(EXPERT_TOKENS, n_tok)
    ec = EXPERT_CHUNK
    key_rows = ec // PEER_N_KEYS
    per_tok = pl.BlockSpec((PEER_HEADS, PEER_N_KEYS, tt), lambda i, c: (0, 0, i))
    per_row = pl.BlockSpec((PEER_HEADS, key_rows, tt), lambda i, c: (0, c, i))
    return pl.pallas_call(
        _expert_kernel,
        grid=(n_tok // tt, n_exp // ec),
        in_specs=[
            pl.BlockSpec((tt, d), lambda i, c: (i, 0)),
            pl.BlockSpec((ec, d), lambda i, c: (c, 0)),
            pl.BlockSpec((ec, d), lambda i, c: (c, 0)),
            per_tok, per_tok, per_row, per_row,
        ],
        out_specs=pl.BlockSpec((tt, d), lambda i, c: (i, 0)),
        out_shape=jax.ShapeDtypeStruct((n_tok, d), F32),
        scratch_shapes=[pltpu.VMEM((ec, tt), BF16)],
        compiler_params=_params("arbitrary", "arbitrary"),
        name="peer_experts",
    )(xb, u, v, rank2, e2, nsel, e1)


def _res_norm_kernel(y_ref, x_ref, g_ref, b_ref, o_ref, ob_ref):
    out = _layer_norm(DEEPNORM_ALPHA * x_ref[...] + y_ref[...], g_ref[...], b_ref[...])
    o_ref[...] = out
    ob_ref[...] = out.astype(ob_ref.dtype)


def _res_norm(y, x, g_row, b_row):
    n_tok, d = x.shape
    rows = min(MIX_ROWS, n_tok)
    tile = pl.BlockSpec((rows, d), lambda i: (i, 0))
    vec = pl.BlockSpec((1, d), lambda i: (0, 0))
    return pl.pallas_call(
        _res_norm_kernel,
        grid=(n_tok // rows,),
        in_specs=[tile, tile, vec, vec],
        out_specs=[tile, tile],
        out_shape=[jax.ShapeDtypeStruct((n_tok, d), F32), jax.ShapeDtypeStruct((n_tok, d), BF16)],
        compiler_params=_params("arbitrary"),
        name="res_norm",
    )(y, x, g_row, b_row)


def _rope_tables(seq):
    inv = ROPE_THETA ** (-jnp.arange(0, HEAD_DIM, 2, dtype=F32) / HEAD_DIM)
    ang = jnp.arange(seq, dtype=F32)[:, None] * inv[None, :]
    ang = jnp.concatenate([ang, ang], axis=-1)
    half_sign = jnp.where(jnp.arange(HEAD_DIM) < HEAD_DIM // 2, -1.0, 1.0).astype(F32)
    return jnp.cos(ang), jnp.sin(ang) * half_sign


def kernel(x, w_in, b_gate, w_branch_attn, w_branch_pool, w_pool_group, pool_scale, w_out,
           ln1_g, ln1_b, w_peer_q, peer_subkeys, peer_u, peer_v, ln2_g, ln2_b):
    batch, seq, d = x.shape
    depth = w_in.shape[0]
    assert depth == DEPTH
    cos, sin_signed = _rope_tables(seq)
    xf = x.reshape(batch * seq, d)
    xb = xf.astype(BF16)
    for l in range(depth):
        w_in_l = w_in[l].astype(BF16)
        outs, lses = [], []
        for g in range(len(ATTN_DILATIONS)):
            q, k, v = _qkv_proj(xb, w_in_l, cos, sin_signed, g, batch, seq)
            o, lse = _band_attention(q, k, v)
            outs.append(o)
            lses.append(lse)
        pooled = _pool_proj(xb, w_in_l[:, 3 * ATTN_WIDTH:3 * ATTN_WIDTH + d // 2], seq)
        gates = _gate_proj(xb, w_in_l, b_gate[l].reshape(1, 2 * d))
        merged = _merge(outs, lses, pooled, gates, w_branch_attn[l].astype(BF16),
                        w_pool_group[l].astype(BF16), pool_scale[l].reshape(1, -1),
                        w_branch_pool[l].astype(BF16), batch, seq)
        xf, xb = _out_norm(merged, xf, w_out[l].astype(BF16), ln1_g[l].reshape(1, d), ln1_b[l].reshape(1, d))
        rank2, e2, nsel, e1 = _route(xb, w_peer_q[l].astype(BF16), peer_subkeys[l].astype(BF16))
        y = _experts(xb, peer_u[l].astype(BF16), peer_v[l].astype(BF16), rank2, e2, nsel, e1)
        xf, xb = _res_norm(y, xf, ln2_g[l].reshape(1, d), ln2_b[l].reshape(1, d))
    return xf.reshape(batch, seq, d)
```

```python
import functools

import jax
import jax.numpy as jnp
from jax import lax
from jax.experimental import pallas as pl
from jax.experimental.pallas import tpu as pltpu

F32 = jnp.float32
BF16 = jnp.bfloat16

HEAD_DIM = 128
ATTN_DILATIONS = (1, 4, 16)
HEADS_PER_GROUP = 4
GROUP_WIDTH = HEADS_PER_GROUP * HEAD_DIM
ATTN_WIDTH = len(ATTN_DILATIONS) * GROUP_WIDTH
BAND = 128
ROPE_THETA = 10000.0
POOL_WINDOWS = (2, 4, 8, 16)
POOL_HALO = 16
PEER_HEADS = 8
PEER_N_KEYS = 128
PEER_HALF = 128
PEER_TOPK = 16
LN_EPS = 1e-5
DEPTH = 2
DEEPNORM_ALPHA = (2 * DEPTH) ** 0.25
NEG = -1e30

VMEM_LIMIT = 56 * 1024 * 1024
PROJ_ROWS = 1024
MIX_ROWS = 512
ATTN_ROWS = 512
ROUTE_TOKENS = 512
ROUTE_LANES = 128
EXPERT_TOKENS = 512
EXPERT_CHUNK = 1024


def _params(*sem):
    return pltpu.CompilerParams(dimension_semantics=sem, vmem_limit_bytes=VMEM_LIMIT)


def _dot(a, b):
    return jnp.dot(a, b, preferred_element_type=F32)


def _dot_nt(a, b):
    return lax.dot_general(a, b, (((1,), (1,)), ((), ())), preferred_element_type=F32)


def _dot_tn(a, b):
    return lax.dot_general(a, b, (((0,), (0,)), ((), ())), preferred_element_type=F32)


def _layer_norm(z, g, b):
    mu = jnp.mean(z, axis=-1, keepdims=True)
    zc = z - mu
    var = jnp.mean(zc * zc, axis=-1, keepdims=True)
    return zc * lax.rsqrt(var + LN_EPS) * g + b


def _qkv_kernel(x_ref, w_ref, cos_ref, sin_ref, q_ref, k_ref, v_ref, acc_ref, *, r, rows):
    j = pl.program_id(1)
    res = _dot(x_ref[...], w_ref[...])
    for h in range(HEADS_PER_GROUP):
        acc_ref[h] = res[:, h * HEAD_DIM:(h + 1) * HEAD_DIM]
    n = rows // r

    def phase_rows(ref, p):
        return ref[...] if r == 1 else ref[pl.ds(p, n, r), :]

    def store(o_ref, rope, scale):
        for p in range(r):
            if rope:
                c = phase_rows(cos_ref, p)
                s = phase_rows(sin_ref, p)
            for h in range(HEADS_PER_GROUP):
                t = phase_rows(acc_ref.at[h], p)
                if rope:
                    t = t * c + pltpu.roll(t, HEAD_DIM // 2, 1) * s
                if scale != 1.0:
                    t = t * scale
                o_ref[0, p, :, h * HEAD_DIM:(h + 1) * HEAD_DIM] = t.astype(o_ref.dtype)

    @pl.when(j == 0)
    def _():
        store(q_ref, True, HEAD_DIM ** -0.5)

    @pl.when(j == 1)
    def _():
        store(k_ref, True, 1.0)

    @pl.when(j == 2)
    def _():
        store(v_ref, False, 1.0)


def _qkv_proj(xb, w_in, cos, sin_signed, group, batch, seq):
    n_tok, d = xb.shape
    r = ATTN_DILATIONS[group]
    rows = min(PROJ_ROWS, seq)
    tps = seq // rows
    n_groups = len(ATTN_DILATIONS)
    out = jax.ShapeDtypeStruct((batch, r, seq // r, GROUP_WIDTH), BF16)
    o_spec = pl.BlockSpec((1, r, rows // r, GROUP_WIDTH), lambda i, j: (i // tps, 0, i % tps, 0))
    return pl.pallas_call(
        functools.partial(_qkv_kernel, r=r, rows=rows),
        grid=(n_tok // rows, 3),
        in_specs=[
            pl.BlockSpec((rows, d), lambda i, j: (i, 0)),
            pl.BlockSpec((d, GROUP_WIDTH), lambda i, j: (0, group + n_groups * j)),
            pl.BlockSpec((rows, HEAD_DIM), lambda i, j: (i % tps, 0)),
            pl.BlockSpec((rows, HEAD_DIM), lambda i, j: (i % tps, 0)),
        ],
        out_specs=[o_spec, o_spec, o_spec],
        out_shape=[out, out, out],
        scratch_shapes=[pltpu.VMEM((HEADS_PER_GROUP, rows, HEAD_DIM), F32)],
        compiler_params=_params("arbitrary", "arbitrary"),
        name=f"qkv_proj_g{group}",
    )(xb, w_in, cos, sin_signed)


def _pool_kernel(x_ref, w_ref, o_ref, ext_ref, *, rows, tps):
    i = pl.program_id(0)
    it = i % tps

    @pl.when(it == 0)
    def _():
        ext_ref[0:POOL_HALO, :] = jnp.zeros((POOL_HALO, ext_ref.shape[1]), F32)

    ext_ref[POOL_HALO:POOL_HALO + rows, :] = _dot(x_ref[...], w_ref[...])
    pos = (it * rows + lax.broadcasted_iota(jnp.int32, (rows, 1), 0)).astype(F32)
    gw = ext_ref.shape[1] // len(POOL_WINDOWS)
    for g, w in enumerate(POOL_WINDOWS):
        cols = slice(g * gw, (g + 1) * gw)
        tok = ext_ref[POOL_HALO:POOL_HALO + rows, cols]
        acc = tok
        for dlt in range(1, w):
            acc = acc + ext_ref[POOL_HALO - dlt:POOL_HALO - dlt + rows, cols]
        inv = 1.0 / jnp.minimum(pos + 1.0, float(w))
        o_ref[:, cols] = (acc * inv - tok).astype(o_ref.dtype)
    ext_ref[0:POOL_HALO, :] = ext_ref[rows:rows + POOL_HALO, :]


def _pool_proj(xb, w_pool, seq):
    n_tok, d = xb.shape
    pw = w_pool.shape[1]
    rows = min(PROJ_ROWS, seq)
    return pl.pallas_call(
        functools.partial(_pool_kernel, rows=rows, tps=seq // rows),
        grid=(n_tok // rows,),
        in_specs=[
            pl.BlockSpec((rows, d), lambda i: (i, 0)),
            pl.BlockSpec((d, pw), lambda i: (0, 0)),
        ],
        out_specs=pl.BlockSpec((rows, pw), lambda i: (i, 0)),
        out_shape=jax.ShapeDtypeStruct((n_tok, pw), BF16),
        scratch_shapes=[pltpu.VMEM((rows + POOL_HALO, pw), F32)],
        compiler_params=_params("arbitrary"),
        name="pool_proj",
    )(xb, w_pool)


def _gate_kernel(x_ref, w_ref, b_ref, o_ref):
    z = _dot(x_ref[...], w_ref[...]) + b_ref[...]
    o_ref[...] = jax.nn.sigmoid(z).astype(o_ref.dtype)


def _gate_proj(xb, w_in, b_gate_row):
    n_tok, d = xb.shape
    rows = min(PROJ_ROWS, n_tok)
    tn = 512
    col0 = (3 * ATTN_WIDTH + d // 2) // tn
    assert col0 * tn == 3 * ATTN_WIDTH + d // 2 and (2 * d) % tn == 0
    return pl.pallas_call(
        _gate_kernel,
        grid=(n_tok // rows, 2 * d // tn),
        in_specs=[
            pl.BlockSpec((rows, d), lambda i, j: (i, 0)),
            pl.BlockSpec((d, tn), lambda i, j: (0, col0 + j)),
            pl.BlockSpec((1, tn), lambda i, j: (0, j)),
        ],
        out_specs=pl.BlockSpec((rows, tn), lambda i, j: (i, j)),
        out_shape=jax.ShapeDtypeStruct((n_tok, 2 * d), BF16),
        compiler_params=_params("arbitrary", "arbitrary"),
        name="gate_proj",
    )(xb, w_in, b_gate_row)


def _attn_kernel(q_ref, kp_ref, kc_ref, vp_ref, vc_ref, o_ref, lse_ref, kf_ref, vf_ref, *, qb):
    n = pl.program_id(2)
    kf_ref[0:BAND, :] = kp_ref[0, 0]
    kf_ref[BAND:BAND + qb, :] = kc_ref[0, 0]
    vf_ref[0:BAND, :] = vp_ref[0, 0]
    vf_ref[BAND:BAND + qb, :] = vc_ref[0, 0]
    row = lax.broadcasted_iota(jnp.int32, (BAND, 2 * BAND), 0)
    col = lax.broadcasted_iota(jnp.int32, (BAND, 2 * BAND), 1)
    dist = row + BAND - col
    band = (dist >= 0) & (dist <= BAND)
    first = band & (col >= jnp.where(n > 0, 0, BAND))
    lane = lax.broadcasted_iota(jnp.int32, (BAND, HEAD_DIM), 1)
    for t in range(qb // BAND):
        mask = first if t == 0 else band
        lse_blk = jnp.zeros((BAND, HEAD_DIM), F32)
        for h in range(HEADS_PER_GROUP):
            cols = slice(h * HEAD_DIM, (h + 1) * HEAD_DIM)
            q = q_ref[0, 0, t * BAND:(t + 1) * BAND, cols]
            kk = kf_ref[t * BAND:(t + 2) * BAND, cols]
            vv = vf_ref[t * BAND:(t + 2) * BAND, cols]
            s = jnp.where(mask, _dot_nt(q, kk), NEG)
            m = jnp.max(s, axis=-1, keepdims=True)
            p = jnp.exp(s - m)
            den = jnp.sum(p, axis=-1, keepdims=True)
            o = _dot(p.astype(BF16), vv) * (1.0 / den)
            o_ref[0, 0, t * BAND:(t + 1) * BAND, cols] = o.astype(o_ref.dtype)
            lse_blk = jnp.where(lane == h, m + jnp.log(den), lse_blk)
        lse_ref[0, 0, t * BAND:(t + 1) * BAND, :] = lse_blk


def _band_attention(q, k, v):
    batch, r, length, _ = q.shape
    qb = min(ATTN_ROWS, length)
    sub = qb // BAND
    cur = pl.BlockSpec((1, 1, qb, GROUP_WIDTH), lambda b, p, n: (b, p, n, 0))
    prev = pl.BlockSpec((1, 1, BAND, GROUP_WIDTH), lambda b, p, n: (b, p, jnp.maximum(n * sub - 1, 0), 0))
    return pl.pallas_call(
        functools.partial(_attn_kernel, qb=qb),
        grid=(batch, r, length // qb),
        in_specs=[cur, prev, cur, prev, cur],
        out_specs=[cur, pl.BlockSpec((1, 1, qb, HEAD_DIM), lambda b, p, n: (b, p, n, 0))],
        out_shape=[jax.ShapeDtypeStruct(q.shape, BF16),
                   jax.ShapeDtypeStruct((batch, r, length, HEAD_DIM), F32)],
        scratch_shapes=[pltpu.VMEM((BAND + qb, GROUP_WIDTH), BF16),
                        pltpu.VMEM((BAND + qb, GROUP_WIDTH), BF16)],
        compiler_params=_params("arbitrary", "arbitrary", "arbitrary"),
        name=f"band_attention_r{r}",
    )(q, k, k, v, v)


def _merge_kernel(o0_ref, o1_ref, o2_ref, l0_ref, l1_ref, l2_ref, pooled_ref, gates_ref,
                  wa_ref, wg_ref, ps_ref, wb_ref, out_ref, on1_ref, on2_ref, ln1_ref, ln2_ref, *, rows):
    for o_ref, l_ref, on_ref, ln_ref, r in ((o1_ref, l1_ref, on1_ref, ln1_ref, ATTN_DILATIONS[1]),
                                            (o2_ref, l2_ref, on2_ref, ln2_ref, ATTN_DILATIONS[2])):
        n = rows // r
        for p in range(r):
            op = o_ref[0, p].astype(F32)
            for h in range(HEADS_PER_GROUP):
                on_ref[h, pl.ds(p, n, r), :] = op[:, h * HEAD_DIM:(h + 1) * HEAD_DIM]
            ln_ref[pl.ds(p, n, r), :] = l_ref[0, p]
    l0, l1, l2 = l0_ref[0, 0], ln1_ref[...], ln2_ref[...]
    m = jnp.maximum(jnp.maximum(l0, l1), l2)
    e0, e1, e2 = jnp.exp(l0 - m), jnp.exp(l1 - m), jnp.exp(l2 - m)
    inv = 1.0 / (e0 + e1 + e2)
    w0, w1, w2 = e0 * inv, e1 * inv, e2 * inv
    parts = []
    for h in range(HEADS_PER_GROUP):
        cols = slice(h * HEAD_DIM, (h + 1) * HEAD_DIM)
        parts.append(w0[:, h:h + 1] * o0_ref[0, 0, :, cols].astype(F32)
                     + w1[:, h:h + 1] * on1_ref[h]
                     + w2[:, h:h + 1] * on2_ref[h])
    attn = jnp.concatenate(parts, axis=-1).astype(BF16)
    branch_a = _dot(attn, wa_ref[...])
    gw = wg_ref.shape[1]
    mixed = jnp.concatenate(
        [_dot(pooled_ref[:, g * gw:(g + 1) * gw], wg_ref[g]) for g in range(len(POOL_WINDOWS))], axis=-1)
    branch_b = _dot((mixed * ps_ref[...]).astype(BF16), wb_ref[...])
    d = branch_a.shape[1]
    merged = gates_ref[:, 0:d].astype(F32) * branch_a + gates_ref[:, d:2 * d].astype(F32) * branch_b
    out_ref[...] = merged.astype(out_ref.dtype)


def _merge(outs, lses, pooled, gates, w_ba, w_pg, pool_scale_row, w_bp, batch, seq):
    n_tok = pooled.shape[0]
    d = w_ba.shape[1]
    rows = min(MIX_ROWS, seq)
    tps = seq // rows

    def strided(r, width):
        return pl.BlockSpec((1, r, rows // r, width), lambda i: (i // tps, 0, i % tps, 0))

    def whole(a):
        return pl.BlockSpec(a.shape, lambda i: (0,) * a.ndim)

    return pl.pallas_call(
        functools.partial(_merge_kernel, rows=rows),
        grid=(n_tok // rows,),
        in_specs=[strided(r, GROUP_WIDTH) for r in ATTN_DILATIONS]
        + [strided(r, HEAD_DIM) for r in ATTN_DILATIONS]
        + [pl.BlockSpec((rows, pooled.shape[1]), lambda i: (i, 0)),
           pl.BlockSpec((rows, gates.shape[1]), lambda i: (i, 0)),
           whole(w_ba), whole(w_pg), whole(pool_scale_row), whole(w_bp)],
        out_specs=pl.BlockSpec((rows, d), lambda i: (i, 0)),
        out_shape=jax.ShapeDtypeStruct((n_tok, d), BF16),
        scratch_shapes=[pltpu.VMEM((HEADS_PER_GROUP, rows, HEAD_DIM), F32),
                        pltpu.VMEM((HEADS_PER_GROUP, rows, HEAD_DIM), F32),
                        pltpu.VMEM((rows, HEAD_DIM), F32), pltpu.VMEM((rows, HEAD_DIM), F32)],
        compiler_params=_params("arbitrary"),
        name="merge",
    )(*outs, *lses, pooled, gates, w_ba, w_pg, pool_scale_row, w_bp)


def _out_norm_kernel(m_ref, x_ref, w_ref, g_ref, b_ref, o_ref, ob_ref):
    z = DEEPNORM_ALPHA * x_ref[...] + _dot(m_ref[...], w_ref[...])
    out = _layer_norm(z, g_ref[...], b_ref[...])
    o_ref[...] = out
    ob_ref[...] = out.astype(ob_ref.dtype)


def _out_norm(merged, x, w_out, g_row, b_row):
    n_tok, d = x.shape
    rows = min(MIX_ROWS, n_tok)
    tile = pl.BlockSpec((rows, d), lambda i: (i, 0))
    vec = pl.BlockSpec((1, d), lambda i: (0, 0))
    return pl.pallas_call(
        _out_norm_kernel,
        grid=(n_tok // rows,),
        in_specs=[tile, tile, pl.BlockSpec((d, d), lambda i: (0, 0)), vec, vec],
        out_specs=[tile, tile],
        out_shape=[jax.ShapeDtypeStruct((n_tok, d), F32), jax.ShapeDtypeStruct((n_tok, d), BF16)],
        compiler_params=_params("arbitrary"),
        name="out_norm",
    )(merged, x, w_out, g_row, b_row)


MARK = -(2.0 ** 100)
MARK_STEP = 2.0 ** -5


def _top_rows(arrays, k, exact_ties):
    nrows, t = arrays[0].shape
    kid = lax.broadcasted_iota(jnp.int32, (k, t), 0)
    rid = lax.broadcasted_iota(jnp.int32, (nrows, t), 0).astype(F32) if exact_ties else None

    def pick(i, cur, vals):
        mx = jnp.max(cur, axis=0, keepdims=True)
        hit = cur == mx
        if exact_ties:
            hit = rid == jnp.min(jnp.where(hit, rid, float(nrows)), axis=0, keepdims=True)
        mark = MARK * (1.0 + i.astype(F32) * MARK_STEP)
        return jnp.where(hit, mark, cur), jnp.where(kid == i, mx, vals)

    def body(i, carry):
        return tuple(pick(i, *c) for c in carry)

    out = lax.fori_loop(0, k, body, tuple((a, jnp.zeros((k, t), F32)) for a in arrays))
    return [(vals, cur) for cur, vals in out]


def _mark_rank(marked, orig, k):
    return jnp.where(marked != orig, (marked * (1.0 / MARK) - 1.0) * (1.0 / MARK_STEP), float(k))


def _miscounted(pairs, k):
    bad = None
    for marked, orig in pairs:
        cnt = jnp.sum(jnp.where(marked != orig, 1.0, 0.0), axis=0, keepdims=True)
        dev = jnp.abs(cnt - float(k))
        bad = dev if bad is None else jnp.maximum(bad, dev)
    return jnp.max(bad) > 0.0


def _pair_rows(p, q, op, fill):
    t = p.shape[1]
    row8 = lax.broadcasted_iota(jnp.int32, (8, t), 0)
    pieces = [op(p[0:1], q[0:16]), op(p[1:2], q[0:8])]
    for i in range(2, 8):
        pieces.append(jnp.where(row8 < PEER_TOPK // (i + 1), op(p[i:i + 1], q[0:8]), fill))
    pieces.append(op(p[8:16], q[0:1]))
    return jnp.concatenate(pieces, axis=0)


def _route_kernel(x_ref, wq_ref, sk_ref, r2_ref, e2_ref, n1_ref, e1_ref, s_ref, m_ref, top_ref, cm_ref):
    assert PEER_TOPK == 16
    k = PEER_TOPK
    t = s_ref.shape[2]
    qf = _dot(x_ref[...], wq_ref[...]).astype(BF16)
    for c in range(2):
        s_ref[c] = _dot_nt(sk_ref[0, c], qf[:, c * PEER_HALF:(c + 1) * PEER_HALF])
    blocks = [slice(b * ROUTE_LANES, (b + 1) * ROUTE_LANES) for b in range(t // ROUTE_LANES)]

    def half_tops(cols, exact_ties):
        res = _top_rows([s_ref[0, :, cols], s_ref[1, :, cols]], k, exact_ties)
        for c, (vals, marked) in enumerate(res):
            top_ref[c, :, cols] = vals
            m_ref[c, :, cols] = marked

    for cols in blocks:
        half_tops(cols, False)
        tied = _miscounted([(m_ref[c, :, cols], s_ref[c, :, cols]) for c in range(2)], k)
        pl.when(tied)(functools.partial(half_tops, cols, True))

    cand = _pair_rows(top_ref[0], top_ref[1], jnp.add, -jnp.inf)

    def joint_tops(exact_ties):
        cm_ref[...] = _top_rows([cand], k, exact_ties)[0][1]

    joint_tops(False)
    pl.when(_miscounted([(cm_ref[...], cand)], k))(functools.partial(joint_tops, True))

    for cols in blocks:
        p_val, q_val = top_ref[0, :, cols], top_ref[1, :, cols]
        n = p_val.shape[1]
        chosen = jnp.where(cm_ref[:, cols] != cand[:, cols], 1.0, 0.0)
        ep = jnp.exp(p_val - p_val[0:1])
        eq = jnp.exp(q_val - q_val[0:1])
        z = jnp.sum(chosen * _pair_rows(ep, eq, jnp.multiply, 0.0), axis=0, keepdims=True)
        nsel_sorted = [jnp.sum(chosen[0:16], axis=0, keepdims=True)]
        nsel_sorted += [jnp.sum(chosen[8 + 8 * i:16 + 8 * i], axis=0, keepdims=True) for i in range(1, 8)]
        nsel_sorted += [chosen[72 + i:73 + i] for i in range(8)]
        s1, s2 = s_ref[0, :, cols], s_ref[1, :, cols]
        rank1 = _mark_rank(m_ref[0, :, cols], s1, k)
        nsel = jnp.zeros((PEER_N_KEYS, n), F32)
        for i in range(k):
            nsel = jnp.where(rank1 == float(i), nsel_sorted[i], nsel)
        r2_ref[0, :, cols] = _mark_rank(m_ref[1, :, cols], s2, k)
        e2_ref[0, :, cols] = jnp.exp(s2 - q_val[0:1])
        n1_ref[0, :, cols] = nsel
        e1_ref[0, :, cols] = jnp.exp(s1 - p_val[0:1]) * (1.0 / z)


def _route(xb, w_q, subkeys):
    n_tok, d = xb.shape
    tt = min(ROUTE_TOKENS, n_tok)
    out = jax.ShapeDtypeStruct((PEER_HEADS, PEER_N_KEYS, n_tok), F32)
    o_spec = pl.BlockSpec((1, PEER_N_KEYS, tt), lambda i, h: (h, 0, i))
    return pl.pallas_call(
        _route_kernel,
        grid=(n_tok // tt, PEER_HEADS),
        in_specs=[
            pl.BlockSpec((tt, d), lambda i, h: (i, 0)),
            pl.BlockSpec((d, 2 * PEER_HALF), lambda i, h: (0, h)),
            pl.BlockSpec((1, 2, PEER_N_KEYS, PEER_HALF), lambda i, h: (h, 0, 0, 0)),
        ],
        out_specs=[o_spec] * 4,
        out_shape=[out] * 4,
        scratch_shapes=[pltpu.VMEM((2, PEER_N_KEYS, tt), F32), pltpu.VMEM((2, PEER_N_KEYS, tt), F32),
                        pltpu.VMEM((2, PEER_TOPK, tt), F32), pltpu.VMEM((80, tt), F32)],
        compiler_params=_params("arbitrary", "arbitrary"),
        name="peer_route",
    )(xb, w_q, subkeys)


def _expert_kernel(x_ref, u_ref, v_ref, r2_ref, e2_ref, n1_ref, e1_ref, y_ref,
                   ht0_ref, ht1_ref, at0_ref, at1_ref, *, nc):
    t = pl.program_id(0)
    hts, ats = (ht0_ref, ht1_ref), (at0_ref, at1_ref)

    @pl.when(t == 0)
    def _():
        for ref in hts + ats + (y_ref,):
            ref[...] = jnp.zeros(ref.shape, ref.dtype)

    restart = jnp.logical_and(t >= 2, (t - 2) % nc == 0)

    def step(slot):
        ht_in, at_out = hts[1 - slot], ats[1 - slot]
        n_rounds = u_ref.shape[0] // PEER_N_KEYS
        ncol = y_ref.shape[1] // n_rounds
        a_t = ats[slot][...].T
        for r in range(n_rounds):
            for s in range(PEER_N_KEYS // 32):
                keys = slice(s * 32, (s + 1) * 32)
                rows = slice(r * PEER_N_KEYS + s * 32, r * PEER_N_KEYS + (s + 1) * 32)
                gate = jnp.zeros((32, ht_in.shape[1]), F32)
                for h in range(PEER_HEADS):
                    keep = r2_ref[h, keys, :] < n1_ref[h, r:r + 1, :]
                    gate = gate + jnp.where(keep, e2_ref[h, keys, :], 0.0) * e1_ref[h, r:r + 1, :]
                hr = ht_in[rows, :]
                act = 0.5 * hr * (1.0 + lax.erf(hr * (2.0 ** -0.5)))
                at_out[rows, :] = (gate * act).astype(at_out.dtype)
            yc = slice(r * ncol, (r + 1) * ncol)
            y_ref[:, yc] = jnp.where(restart, 0.0, y_ref[:, yc]) + _dot(a_t, v_ref[:, yc])
        hts[slot][...] = _dot_nt(u_ref[...], x_ref[...])

    for slot in range(2):
        pl.when(t % 2 == slot)(functools.partial(step, slot))


def _experts(xb, u, v, rank2, e2, nsel, e1):
    n_tok, d = xb.shape
    n_exp = u.shape[0]
    tt = min(EXPERT_TOKENS, n_tok)
    ec = EXPERT_CHUNK
    nc = n_exp // ec
    key_rows = ec // PEER_N_KEYS
    pairs = (n_tok // tt) * nc
    stage1 = lambda t: jnp.minimum(t, pairs - 1)
    stage2 = lambda t: jnp.clip(t - 1, 0, pairs - 1)
    stage3 = lambda t: jnp.clip(t - 2, 0, pairs - 1)
    per_tok = pl.BlockSpec((PEER_HEADS, PEER_N_KEYS, tt), lambda t: (0, 0, stage2(t) // nc))
    per_row = pl.BlockSpec((PEER_HEADS, key_rows, tt), lambda t: (0, stage2(t) % nc, stage2(t) // nc))
    return pl.pallas_call(
        functools.partial(_expert_kernel, nc=nc),
        grid=(pairs + 2,),
        in_specs=[
            pl.BlockSpec((tt, d), lambda t: (stage1(t) // nc, 0)),
            pl.BlockSpec((ec, d), lambda t: (stage1(t) % nc, 0)),
            pl.BlockSpec((ec, d), lambda t: (stage3(t) % nc, 0)),
            per_tok, per_tok, per_row, per_row,
        ],
        out_specs=pl.BlockSpec((tt, d), lambda t: (stage3(t) // nc, 0)),
        out_shape=jax.ShapeDtypeStruct((n_tok, d), F32),
        scratch_shapes=[pltpu.VMEM((ec, tt), F32), pltpu.VMEM((ec, tt), F32),
                        pltpu.VMEM((ec, tt), BF16), pltpu.VMEM((ec, tt), BF16)],
        compiler_params=_params("arbitrary"),
        name="peer_experts",
    )(xb, u, v, rank2, e2, nsel, e1)


def _res_norm_kernel(y_ref, x_ref, g_ref, b_ref, o_ref, ob_ref):
    out = _layer_norm(DEEPNORM_ALPHA * x_ref[...] + y_ref[...], g_ref[...], b_ref[...])
    o_ref[...] = out
    ob_ref[...] = out.astype(ob_ref.dtype)


def _res_norm(y, x, g_row, b_row):
    n_tok, d = x.shape
    rows = min(MIX_ROWS, n_tok)
    tile = pl.BlockSpec((rows, d), lambda i: (i, 0))
    vec = pl.BlockSpec((1, d), lambda i: (0, 0))
    return pl.pallas_call(
        _res_norm_kernel,
        grid=(n_tok // rows,),
        in_specs=[tile, tile, vec, vec],
        out_specs=[tile, tile],
        out_shape=[jax.ShapeDtypeStruct((n_tok, d), F32), jax.ShapeDtypeStruct((n_tok, d), BF16)],
        compiler_params=_params("arbitrary"),
        name="res_norm",
    )(y, x, g_row, b_row)


def _rope_tables(seq):
    inv = ROPE_THETA ** (-jnp.arange(0, HEAD_DIM, 2, dtype=F32) / HEAD_DIM)
    ang = jnp.arange(seq, dtype=F32)[:, None] * inv[None, :]
    ang = jnp.concatenate([ang, ang], axis=-1)
    half_sign = jnp.where(jnp.arange(HEAD_DIM) < HEAD_DIM // 2, -1.0, 1.0).astype(F32)
    return jnp.cos(ang), jnp.sin(ang) * half_sign


def kernel(x, w_in, b_gate, w_branch_attn, w_branch_pool, w_pool_group, pool_scale, w_out,
           ln1_g, ln1_b, w_peer_q, peer_subkeys, peer_u, peer_v, ln2_g, ln2_b):
    batch, seq, d = x.shape
    depth = w_in.shape[0]
    assert depth == DEPTH
    cos, sin_signed = _rope_tables(seq)
    xf = x.reshape(batch * seq, d)
    xb = xf.astype(BF16)
    for l in range(depth):
        w_in_l = w_in[l].astype(BF16)
        outs, lses = [], []
        for g in range(len(ATTN_DILATIONS)):
            q, k, v = _qkv_proj(xb, w_in_l, cos, sin_signed, g, batch, seq)
            o, lse = _band_attention(q, k, v)
            outs.append(o)
            lses.append(lse)
        pooled = _pool_proj(xb, w_in_l[:, 3 * ATTN_WIDTH:3 * ATTN_WIDTH + d // 2], seq)
        gates = _gate_proj(xb, w_in_l, b_gate[l].reshape(1, 2 * d))
        merged = _merge(outs, lses, pooled, gates, w_branch_attn[l].astype(BF16),
                        w_pool_group[l].astype(BF16), pool_scale[l].reshape(1, -1),
                        w_branch_pool[l].astype(BF16), batch, seq)
        xf, xb = _out_norm(merged, xf, w_out[l].astype(BF16), ln1_g[l].reshape(1, d), ln1_b[l].reshape(1, d))
        rank2, e2, nsel, e1 = _route(xb, w_peer_q[l].astype(BF16), peer_subkeys[l].astype(BF16))
        y = _experts(xb, peer_u[l].astype(BF16), peer_v[l].astype(BF16), rank2, e2, nsel, e1)
        xf, xb = _res_norm(y, xf, ln2_g[l].reshape(1, d), ln2_b[l].reshape(1, d))
    return xf.reshape(batch, seq, d)
```
